```python
import math
import jax, jax.numpy as jnp
from jax import lax
import numpy as np

D_MODEL = 1024
BATCH = 16
SEQ = 256
DEPTH = 1
DEC_BATCH = 8
DEC_SEQ = 4096
PAST_LEN = 256

GRID_W = 64
N_HEADS = 4
HEAD_DIM = 64
V_DIM = 2 * HEAD_DIM
QK_WIDTH = N_HEADS * 2 * HEAD_DIM
ATTN_WIDTH = N_HEADS * V_DIM
CONV_WIDTH = D_MODEL - ATTN_WIDTH
CONV_K = 3
IN_WIDTH = 2 * QK_WIDTH + ATTN_WIDTH + 3 * CONV_WIDTH
D_FF = -(-8 * D_MODEL // (3 * 256)) * 256
ROPE_THETA = 10000.0
Q_BLOCK = 128
EPS = 1e-6
ATTN_SCALE = HEAD_DIM ** -0.5

kernel_name = "hybrid_diffattn_shortconv_dit_step"


def rms_norm(x, g):
    xf = x.astype(jnp.float32)
    y = xf * lax.rsqrt(jnp.mean(xf * xf, axis=-1, keepdims=True) + EPS)
    return (y * g.astype(jnp.float32)).astype(x.dtype)


def modulation(cond, w_ada, b_ada):
    m = jax.nn.silu(cond) @ w_ada + b_ada
    return jnp.split(m, 6, axis=-1)


def project(x, shift, scale, g_pre, w_in):
    b, n = x.shape[0], x.shape[1]
    u = rms_norm(x, g_pre) * (1.0 + scale) + shift
    p = u @ w_in
    o = 0
    q = p[..., o:o + QK_WIDTH].reshape(b, n, N_HEADS, 2, HEAD_DIM); o += QK_WIDTH
    k = p[..., o:o + QK_WIDTH].reshape(b, n, N_HEADS, 2, HEAD_DIM); o += QK_WIDTH
    v = p[..., o:o + ATTN_WIDTH].reshape(b, n, N_HEADS, V_DIM); o += ATTN_WIDTH
    bg = p[..., o:o + CONV_WIDTH]; o += CONV_WIDTH
    cg = p[..., o:o + CONV_WIDTH]; o += CONV_WIDTH
    xi = p[..., o:o + CONV_WIDTH]
    return q, k, v, bg, cg, xi


def axial_rotary_tables(n):
    rows = n // GRID_W
    t = jnp.arange(rows * GRID_W)
    row = (t // GRID_W).astype(jnp.float32)
    col = (t % GRID_W).astype(jnp.float32)
    n_freq = HEAD_DIM // 4
    inv = 1.0 / (ROPE_THETA ** (jnp.arange(n_freq, dtype=jnp.float32) / n_freq))
    ar = row[:, None] * inv[None, :]
    ac = col[:, None] * inv[None, :]
    shp = (1, n, 1, 1, n_freq)
    return (jnp.cos(ar).reshape(shp), jnp.sin(ar).reshape(shp),
            jnp.cos(ac).reshape(shp), jnp.sin(ac).reshape(shp))


def _rotate(x, cos, sin):
    h = x.shape[-1] // 2
    x1, x2 = x[..., :h], x[..., h:]
    return jnp.concatenate([x1 * cos - x2 * sin, x2 * cos + x1 * sin], axis=-1)


def apply_axial_rotary(x, tables):
    cr, sr, cc, sc = tables
    xf = x.astype(jnp.float32)
    half = HEAD_DIM // 2
    out = jnp.concatenate([_rotate(xf[..., :half], cr, sr),
                           _rotate(xf[..., half:], cc, sc)], axis=-1)
    return out.astype(x.dtype)


def diff_attention(q, k, v, lam, g_subln, lambda_init):
    b, nq = q.shape[0], q.shape[1]
    nb = nq // Q_BLOCK
    qb = q.reshape(b, nb, Q_BLOCK, N_HEADS, 2, HEAD_DIM).swapaxes(0, 1)
    kf = k.astype(jnp.float32)
    vf = v.astype(jnp.float32)

    def block(qblk):
        s = jnp.einsum('bqhmd,bkhmd->bhmqk', qblk.astype(jnp.float32), kf) * ATTN_SCALE
        p = jax.nn.softmax(s, axis=-1)
        a = p[:, :, 0] - lam * p[:, :, 1]
        return jnp.einsum('bhqk,bkhe->bqhe', a, vf)

    o = lax.map(block, qb)
    o = o.swapaxes(0, 1).reshape(b, nq, N_HEADS, V_DIM)
    o = o * lax.rsqrt(jnp.mean(o * o, axis=-1, keepdims=True) + EPS)
    o = o * g_subln.astype(jnp.float32) * (1.0 - lambda_init)
    return o.reshape(b, nq, ATTN_WIDTH).astype(q.dtype)


def gated_short_conv(bg, cg, xi, conv_w, conv_b):
    z = cg * xi
    zp = jnp.pad(z, ((0, 0), (1, 1), (0, 0)))
    y = conv_w[0] * zp[:, :-2] + conv_w[1] * zp[:, 1:-1] + conv_w[2] * zp[:, 2:] + conv_b
    return bg * y


def finish_layer(x, ao, co, gate_a, shift_f, scale_f, gate_f,
                 w_out, g_attn_post, g_ffn_pre, g_ffn_post, w_ffn_in, w_ffn_out):
    mix = jnp.concatenate([ao, co], axis=-1) @ w_out
    x = x + gate_a * rms_norm(mix, g_attn_post)
    u = rms_norm(x, g_ffn_pre) * (1.0 + scale_f) + shift_f
    gu = u @ w_ffn_in
    h = jax.nn.silu(gu[..., :D_FF]) * gu[..., D_FF:]
    return x + gate_f * rms_norm(h @ w_ffn_out, g_ffn_post)


def setup_inputs(seed: int = 0) -> dict:
    key = jax.random.key(seed)
    ks = jax.random.split(key, 24)
    f32 = jnp.float32
    nrm = lambda k, s, sc: jax.random.normal(k, s, f32) * sc
    gain = lambda k, s: 1.0 + 0.1 * jax.random.normal(k, s, f32)
    return {
        "x_prompt": nrm(ks[0], (BATCH, SEQ, D_MODEL), 1.0),
        "x_sample": nrm(ks[1], (DEC_BATCH, DEC_SEQ, D_MODEL), 1.0),
        "cache_k": nrm(ks[2], (DEC_BATCH, DEPTH, PAST_LEN, N_HEADS, 2 * HEAD_DIM), 1.0),
        "cache_v": nrm(ks[3], (DEC_BATCH, DEPTH, PAST_LEN, N_HEADS, V_DIM), 1.0),
        "c": nrm(ks[4], (DEC_BATCH, D_MODEL), 1.0),
        "c_ctx": nrm(ks[5], (D_MODEL,), 1.0),
        "w_ada": nrm(ks[6], (DEPTH, D_MODEL, 6 * D_MODEL), 0.5 * D_MODEL ** -0.5),
        "b_ada": nrm(ks[7], (DEPTH, 6 * D_MODEL), 0.02),
        "g_attn_pre": gain(ks[8], (DEPTH, D_MODEL)),
        "g_attn_post": gain(ks[9], (DEPTH, D_MODEL)),
        "g_ffn_pre": gain(ks[10], (DEPTH, D_MODEL)),
        "g_ffn_post": gain(ks[11], (DEPTH, D_MODEL)),
        "w_in": nrm(ks[12], (DEPTH, D_MODEL, IN_WIDTH), D_MODEL ** -0.5),
        "conv_w": nrm(ks[13], (DEPTH, CONV_K, CONV_WIDTH), CONV_K ** -0.5),
        "conv_b": nrm(ks[14], (DEPTH, CONV_WIDTH), 0.02),
        "lambda_q1": nrm(ks[15], (DEPTH, HEAD_DIM), 0.1),
        "lambda_k1": nrm(ks[16], (DEPTH, HEAD_DIM), 0.1),
        "lambda_q2": nrm(ks[17], (DEPTH, HEAD_DIM), 0.1),
        "lambda_k2": nrm(ks[18], (DEPTH, HEAD_DIM), 0.1),
        "g_subln": gain(ks[19], (DEPTH, V_DIM)),
        "w_out": nrm(ks[20], (DEPTH, D_MODEL, D_MODEL), D_MODEL ** -0.5),
        "w_ffn_in": nrm(ks[21], (DEPTH, D_MODEL, 2 * D_FF), D_MODEL ** -0.5),
        "w_ffn_out": nrm(ks[22], (DEPTH, D_FF, D_MODEL), D_FF ** -0.5),
    }


def reference(x_prompt, x_sample, cache_k, cache_v, c, c_ctx, w_ada, b_ada,
              g_attn_pre, g_attn_post, g_ffn_pre, g_ffn_post, w_in, conv_w, conv_b,
              lambda_q1, lambda_k1, lambda_q2, lambda_k2, g_subln, w_out,
              w_ffn_in, w_ffn_out):
    bp, sp = x_prompt.shape[0], x_prompt.shape[1]
    bs, ns = x_sample.shape[0], x_sample.shape[1]
    tables = axial_rotary_tables(ns)
    xp, xs = x_prompt, x_sample
    new_k, new_v = [], []
    for l in range(DEPTH):
        lambda_init = 0.8 - 0.6 * math.exp(-0.3 * l)
        lam = (jnp.exp(jnp.sum(lambda_q1[l].astype(jnp.float32) * lambda_k1[l].astype(jnp.float32)))
               - jnp.exp(jnp.sum(lambda_q2[l].astype(jnp.float32) * lambda_k2[l].astype(jnp.float32)))
               + lambda_init)

        sa, ca, ga, sf, cf, gf = modulation(c_ctx, w_ada[l], b_ada[l])
        q, k, v, bg, cg, xi = project(xp, sa, ca, g_attn_pre[l], w_in[l])
        ao = diff_attention(q, k, v, lam, g_subln[l], lambda_init)
        co = gated_short_conv(bg, cg, xi, conv_w[l], conv_b[l])
        xp = finish_layer(xp, ao, co, ga, sf, cf, gf, w_out[l], g_attn_post[l],
                          g_ffn_pre[l], g_ffn_post[l], w_ffn_in[l], w_ffn_out[l])
        new_k.append(k.reshape(bp, sp, N_HEADS, 2 * HEAD_DIM))
        new_v.append(v)

        sa, ca, ga, sf, cf, gf = [m[:, None, :] for m in modulation(c, w_ada[l], b_ada[l])]
        q, k, v, bg, cg, xi = project(xs, sa, ca, g_attn_pre[l], w_in[l])
        q = apply_axial_rotary(q, tables)
        k = apply_axial_rotary(k, tables)
        kc = cache_k[:, l].reshape(bs, cache_k.shape[2], N_HEADS, 2, HEAD_DIM)
        k_all = jnp.concatenate([kc, k], axis=1)
        v_all = jnp.concatenate([cache_v[:, l], v], axis=1)
        ao = diff_attention(q, k_all, v_all, lam, g_subln[l], lambda_init)
        co = gated_short_conv(bg, cg, xi, conv_w[l], conv_b[l])
        xs = finish_layer(xs, ao, co, ga, sf, cf, gf, w_out[l], g_attn_post[l],
                          g_ffn_pre[l], g_ffn_post[l], w_ffn_in[l], w_ffn_out[l])

    state_k = jnp.stack(new_k, axis=1)
    state_v = jnp.stack(new_v, axis=1)
    return (xp, xs, state_k, state_v)
```

```python
import functools
import math

import numpy as np
import jax
import jax.numpy as jnp
from jax import lax
from jax.experimental import pallas as pl
from jax.experimental.pallas import tpu as pltpu

D_MODEL = 1024
GRID_W = 64
N_HEADS = 4
HEAD_DIM = 64
V_DIM = 2 * HEAD_DIM
QK_WIDTH = N_HEADS * 2 * HEAD_DIM
ATTN_WIDTH = N_HEADS * V_DIM
CONV_WIDTH = D_MODEL - ATTN_WIDTH
CONV_K = 3
IN_WIDTH = 2 * QK_WIDTH + ATTN_WIDTH + 3 * CONV_WIDTH
D_FF = -(-8 * D_MODEL // (3 * 256)) * 256
ROPE_THETA = 10000.0
EPS = 1e-6
ATTN_SCALE = HEAD_DIM ** -0.5
Q_PRESCALE = ATTN_SCALE * math.log2(math.e)

V7X_LANES = 128
BF16_SUBLANE_TILE = 16
V7X_VMEM_LIMIT_BYTES = 56 * 1024 * 1024
NEG_BIG = -1e30

F32 = jnp.float32
BF16 = jnp.bfloat16

MOD_SHIFT_A, MOD_SCALE_A, MOD_GATE_A, MOD_SHIFT_F, MOD_SCALE_F, MOD_GATE_F = range(6)
MOD_ROWS = 16


def _params(n_grid_axes):
    return pltpu.CompilerParams(
        dimension_semantics=("arbitrary",) * n_grid_axes,
        vmem_limit_bytes=V7X_VMEM_LIMIT_BYTES)


def _rms(x):
    return lax.rsqrt(jnp.mean(x * x, axis=-1, keepdims=True) + EPS)


def _norm_mod(x, g, scale, shift):
    y = x * _rms(x) * g
    return (y * (1.0 + scale) + shift).astype(BF16)


def _dot(a, b):
    return jnp.dot(a, b, preferred_element_type=F32)


def _mod_kernel(cond_ref, w_ref, b_ref, o_ref):
    c = cond_ref[...]
    a = c * jax.nn.sigmoid(c)
    o_ref[...] = jnp.dot(a, w_ref[...], preferred_element_type=F32,
                         precision=lax.Precision.HIGHEST) + b_ref[...]


def _modulation(cond, w_ada, b_ada):
    n_out = w_ada.shape[1]
    bn = D_MODEL
    return pl.pallas_call(
        _mod_kernel,
        out_shape=jax.ShapeDtypeStruct((MOD_ROWS, n_out), F32),
        grid=(n_out // bn,),
        in_specs=[pl.BlockSpec((MOD_ROWS, D_MODEL), lambda j: (0, 0)),
                  pl.BlockSpec((D_MODEL, bn), lambda j: (0, j)),
                  pl.BlockSpec((1, bn), lambda j: (0, j))],
        out_specs=pl.BlockSpec((MOD_ROWS, bn), lambda j: (0, j)),
        compiler_params=_params(1),
        name="modulation",
    )(cond, w_ada, b_ada.reshape(1, n_out))


def _rope(xf, cos, sin_signed, first_half):
    outs = []
    for j in range(xf.shape[1] // V7X_LANES):
        xj = xf[:, V7X_LANES * j:V7X_LANES * (j + 1)]
        partner = jnp.where(first_half,
                            pltpu.roll(xj, V7X_LANES - 16, 1),
                            pltpu.roll(xj, 16, 1))
        outs.append(xj * cos + partner * sin_signed)
    return jnp.concatenate(outs, axis=1)


def _inproj_kernel(*refs, tm, n_tiles, rotary, emit_state):
    halo = n_tiles > 1
    it = iter(refs)
    x_ref = next(it)
    xprev_ref = next(it) if halo else None
    xnext_ref = next(it) if halo else None
    shift_ref, scale_ref, g_ref, w_ref, cw_ref, cb_ref = (next(it) for _ in range(6))
    cos_ref = next(it) if rotary else None
    sin_ref = next(it) if rotary else None
    q_ref, kT_ref, v_ref, co_ref = (next(it) for _ in range(4))
    kf_ref = next(it) if emit_state else None
    vf_ref = next(it) if emit_state else None
    z_ref = next(it)

    H = BF16_SUBLANE_TILE
    g, scale, shift = g_ref[...], scale_ref[0], shift_ref[0]
    if halo:
        xe = jnp.concatenate([xprev_ref[...], x_ref[...], xnext_ref[...]], axis=0)
        ue = _norm_mod(xe, g, scale, shift)
        u = ue[H:H + tm]
    else:
        ue = u = _norm_mod(x_ref[...], g, scale, shift)

    def proj(lhs, part):
        return _dot(lhs, w_ref[:, part * QK_WIDTH:(part + 1) * QK_WIDTH])

    qf, kf, vf, bg = proj(u, 0), proj(u, 1), proj(u, 2), proj(u, 3)
    if emit_state:
        kf_ref[...] = kf
        vf_ref[...] = vf
    if rotary:
        lane = lax.broadcasted_iota(jnp.int32, (tm, V7X_LANES), 1)
        first_half = (lane & 16) == 0
        cos, sin = cos_ref[...], sin_ref[...]
        qf = _rope(qf, cos, sin, first_half)
        kf = _rope(kf, cos, sin, first_half)
    q_ref[...] = (qf * Q_PRESCALE).astype(BF16)
    kT_ref[...] = kf.T.astype(BF16)
    v_ref[...] = vf.astype(BF16)

    z = proj(ue, 4) * proj(ue, 5)
    if halo:
        i = pl.program_id(1)
        row = lax.broadcasted_iota(jnp.int32, (tm + 2 * H, 1), 0)
        keep = ((row >= H) | (i > 0)) & ((row < tm + H) | (i < n_tiles - 1))
        z_ref[...] = jnp.where(keep, z, 0.0)
    else:
        zeros = jnp.zeros((H, CONV_WIDTH), F32)
        z_ref[0:H] = zeros
        z_ref[H:H + tm] = z
        z_ref[H + tm:tm + 2 * H] = zeros
    cw = cw_ref[...]
    y = (cw[0:1] * z_ref[H - 1:H - 1 + tm] + cw[1:2] * z_ref[H:H + tm]
         + cw[2:3] * z_ref[H + 1:H + 1 + tm] + cb_ref[...])
    co_ref[...] = (bg * y).astype(BF16)


def _inproj(x, mod, mod_row, g, w_in, conv_w, conv_b, rope_tabs, *, tm, emit_state):
    nb, n, _ = x.shape
    n_tiles = n // tm
    halo = n_tiles > 1
    rotary = rope_tabs is not None
    H = BF16_SUBLANE_TILE
    hb = tm // H

    def mod_spec(comp):
        return pl.BlockSpec((1, 1, D_MODEL), lambda b, i: (mod_row(b) * 6 + comp, 0, 0))

    def full(shape):
        return pl.BlockSpec(shape, lambda b, i: (0,) * len(shape))

    in_specs = [pl.BlockSpec((None, tm, D_MODEL), lambda b, i: (b, i, 0))]
    args = [x]
    if halo:
        in_specs += [
            pl.BlockSpec((None, H, D_MODEL), lambda b, i: (b, jnp.maximum(i * hb - 1, 0), 0)),
            pl.BlockSpec((None, H, D_MODEL),
                         lambda b, i: (b, jnp.minimum((i + 1) * hb, n // H - 1), 0))]
        args += [x, x]
    in_specs += [mod_spec(MOD_SHIFT_A), mod_spec(MOD_SCALE_A), full((1, D_MODEL)),
                 full((D_MODEL, IN_WIDTH)), full((CONV_K, CONV_WIDTH)), full((1, CONV_WIDTH))]
    args += [mod, mod, g.reshape(1, D_MODEL), w_in, conv_w, conv_b.reshape(1, CONV_WIDTH)]
    if rotary:
        in_specs += [pl.BlockSpec((tm, V7X_LANES), lambda b, i: (i, 0))] * 2
        args += list(rope_tabs)

    tok = lambda width, dt: jax.ShapeDtypeStruct((nb, n, width), dt)
    tok_spec = lambda width: pl.BlockSpec((None, tm, width), lambda b, i: (b, i, 0))
    out_shape = [tok(QK_WIDTH, BF16),
                 jax.ShapeDtypeStruct((nb, n_tiles, QK_WIDTH, tm), BF16),
                 tok(ATTN_WIDTH, BF16), tok(CONV_WIDTH, BF16)]
    out_specs = [tok_spec(QK_WIDTH),
                 pl.BlockSpec((None, None, QK_WIDTH, tm), lambda b, i: (b, i, 0, 0)),
                 tok_spec(ATTN_WIDTH), tok_spec(CONV_WIDTH)]
    if emit_state:
        out_shape += [tok(QK_WIDTH, F32), tok(ATTN_WIDTH, F32)]
        out_specs += [tok_spec(QK_WIDTH), tok_spec(ATTN_WIDTH)]

    return pl.pallas_call(
        functools.partial(_inproj_kernel, tm=tm, n_tiles=n_tiles, rotary=rotary,
                          emit_state=emit_state),
        out_shape=out_shape,
        grid=(nb, n_tiles),
        in_specs=in_specs,
        out_specs=out_specs,
        scratch_shapes=[pltpu.VMEM((tm + 2 * H, CONV_WIDTH), F32)],
        compiler_params=_params(2),
        name="inproj_latent" if rotary else "inproj_context",
    )(*args)


def _attn_kernel(*refs, tq, n_chunks, kc, has_cache, lambda_init):
    it = iter(refs)
    q_ref, kT_ref, v_ref = next(it), next(it), next(it)
    ck_ref = next(it) if has_cache else None
    cv_ref = next(it) if has_cache else None
    (co_ref, x_ref, gate_ref, lam_ref, gsub_ref, gpost_ref, wout_ref, o_ref) = (
        next(it) for _ in range(8))

    lp = lam_ref[...]
    lam = (jnp.exp(jnp.sum(lp[0:1] * lp[1:2], axis=-1, keepdims=True))
           - jnp.exp(jnp.sum(lp[2:3] * lp[3:4], axis=-1, keepdims=True)) + lambda_init)

    lane = lax.broadcasted_iota(jnp.int32, (tq, V7X_LANES), 1)
    map0 = lane < HEAD_DIM
    gsub = gsub_ref[...] * (1.0 - lambda_init)

    heads = []
    for h in range(N_HEADS):
        cols = slice(V_DIM * h, V_DIM * (h + 1))
        qh = q_ref[:, cols]
        zero = jnp.zeros_like(qh)
        q2 = jnp.concatenate([jnp.where(map0, qh, zero), jnp.where(map0, zero, qh)], axis=0)

        def update(carry, s, v_c):
            m, l, acc = carry
            m_new = jnp.maximum(m, jnp.max(s, axis=-1, keepdims=True))
            p = jnp.exp2(s - m_new)
            alpha = jnp.exp2(m - m_new)
            l = alpha * l + jnp.sum(p, axis=-1, keepdims=True)
            acc = alpha * acc + _dot(p.astype(BF16), v_c)
            return m_new, l, acc

        carry = (jnp.full((2 * tq, 1), NEG_BIG, F32), jnp.zeros((2 * tq, 1), F32),
                 jnp.zeros((2 * tq, V_DIM), F32))
        if has_cache:
            s = lax.dot_general(q2, ck_ref[:, cols].astype(BF16), (((1,), (1,)), ((), ())),
                                preferred_element_type=F32)
            carry = update(carry, s, cv_ref[:, cols].astype(BF16))

        def body(c, carry):
            s = _dot(q2, kT_ref[c, cols, :])
            start = pl.multiple_of(c * kc, kc)
            return update(carry, s, v_ref[pl.ds(start, kc), cols])

        m, l, acc = lax.fori_loop(0, n_chunks, body, carry)
        on = acc / l
        o = on[:tq] - lam * on[tq:]
        heads.append((o * _rms(o) * gsub).astype(BF16))

    mix_in = jnp.concatenate(heads + [co_ref[...]], axis=1)
    mix = _dot(mix_in, wout_ref[...])
    o_ref[...] = x_ref[...] + gate_ref[0] * (mix * _rms(mix) * gpost_ref[...])


def _attention(q, kT, v, cache, co, x, mod, mod_row, lam_params, g_subln, g_post, w_out,
               *, tq, lambda_init):
    nb, n, _ = q.shape
    n_chunks, kc = kT.shape[1], kT.shape[3]
    has_cache = cache is not None

    def full(shape):
        return pl.BlockSpec(shape, lambda b, i: (0,) * len(shape))

    tok_spec = lambda width: pl.BlockSpec((None, tq, width), lambda b, i: (b, i, 0))
    in_specs = [tok_spec(QK_WIDTH),
                pl.BlockSpec((None, n_chunks, QK_WIDTH, kc), lambda b, i: (b, 0, 0, 0)),
                pl.BlockSpec((None, n, ATTN_WIDTH), lambda b, i: (b, 0, 0))]
    args = [q, kT, v]
    if has_cache:
        past = cache[0].shape[1]
        in_specs += [pl.BlockSpec((None, past, QK_WIDTH), lambda b, i: (b, 0, 0)),
                     pl.BlockSpec((None, past, ATTN_WIDTH), lambda b, i: (b, 0, 0))]
        args += list(cache)
    in_specs += [tok_spec(CONV_WIDTH), tok_spec(D_MODEL),
                 pl.BlockSpec((1, 1, D_MODEL), lambda b, i: (mod_row(b) * 6 + MOD_GATE_A, 0, 0)),
                 full((4, HEAD_DIM)), full((1, V_DIM)), full((1, D_MODEL)),
                 full((D_MODEL, D_MODEL))]
    args += [co, x, mod, lam_params, g_subln.reshape(1, V_DIM), g_post.reshape(1, D_MODEL), w_out]

    return pl.pallas_call(
        functools.partial(_attn_kernel, tq=tq, n_chunks=n_chunks, kc=kc, has_cache=has_cache,
                          lambda_init=lambda_init),
        out_shape=jax.ShapeDtypeStruct((nb, n, D_MODEL), F32),
        grid=(nb, n // tq),
        in_specs=in_specs,
        out_specs=tok_spec(D_MODEL),
        compiler_params=_params(2),
        name="attention_latent" if has_cache else "attention_context",
    )(*args)


FF_CHUNK = 256


def _ffn_kernel(x_ref, shift_ref, scale_ref, gate_ref, gpre_ref, gpost_ref, win_ref, wout_ref,
                o_ref, h_ref):
    x = x_ref[...]
    u = _norm_mod(x, gpre_ref[...], scale_ref[0], shift_ref[0])
    for c in range(D_FF // FF_CHUNK):
        lo = c * FF_CHUNK
        a = _dot(u, win_ref[:, lo:lo + FF_CHUNK])
        b = _dot(u, win_ref[:, D_FF + lo:D_FF + lo + FF_CHUNK])
        h_ref[:, lo:lo + FF_CHUNK] = (a * jax.nn.sigmoid(a) * b).astype(BF16)
    y = _dot(h_ref[...], wout_ref[...])
    o_ref[...] = x + gate_ref[0] * (y * _rms(y) * gpost_ref[...])


def _ffn(x, mod, mod_row, g_pre, g_post, w_ffn_in, w_ffn_out, *, tm):
    nb, n, _ = x.shape

    def mod_spec(comp):
        return pl.BlockSpec((1, 1, D_MODEL), lambda b, i: (mod_row(b) * 6 + comp, 0, 0))

    def full(shape):
        return pl.BlockSpec(shape, lambda b, i: (0,) * len(shape))

    tok_spec = pl.BlockSpec((None, tm, D_MODEL), lambda b, i: (b, i, 0))
    return pl.pallas_call(
        _ffn_kernel,
        out_shape=jax.ShapeDtypeStruct((nb, n, D_MODEL), F32),
        grid=(nb, n // tm),
        in_specs=[tok_spec, mod_spec(MOD_SHIFT_F), mod_spec(MOD_SCALE_F), mod_spec(MOD_GATE_F),
                  full((1, D_MODEL)), full((1, D_MODEL)),
                  full((D_MODEL, 2 * D_FF)), full((D_FF, D_MODEL))],
        out_specs=tok_spec,
        scratch_shapes=[pltpu.VMEM((tm, D_FF), BF16)],
        compiler_params=_params(2),
        name="ffn",
    )(x, mod, mod, mod, g_pre.reshape(1, D_MODEL), g_post.reshape(1, D_MODEL),
      w_ffn_in, w_ffn_out)


def _rope_tables(n):
    t = np.arange(n)
    n_freq = HEAD_DIM // 4
    inv = 1.0 / (ROPE_THETA ** (np.arange(n_freq, dtype=np.float64) / n_freq))
    ar = (t // GRID_W)[:, None] * inv[None, :]
    ac = (t % GRID_W)[:, None] * inv[None, :]
    cos = np.concatenate([np.cos(ar), np.cos(ar), np.cos(ac), np.cos(ac)], axis=1)
    sin = np.concatenate([-np.sin(ar), np.sin(ar), -np.sin(ac), np.sin(ac)], axis=1)
    reps = V7X_LANES // HEAD_DIM
    return (jnp.asarray(np.tile(cos, (1, reps)), F32), jnp.asarray(np.tile(sin, (1, reps)), F32))


def kernel(x_prompt, x_sample, cache_k, cache_v, c, c_ctx, w_ada, b_ada, g_attn_pre, g_attn_post,
           g_ffn_pre, g_ffn_post, w_in, conv_w, conv_b, lambda_q1, lambda_k1, lambda_q2,
           lambda_k2, g_subln, w_out, w_ffn_in, w_ffn_out):
    depth = w_ada.shape[0]
    bp, sp, _ = x_prompt.shape
    bs, ns, _ = x_sample.shape
    past = cache_k.shape[2]
    ctx_row = bs
    cond = jnp.zeros((MOD_ROWS, D_MODEL), F32).at[:bs].set(c).at[ctx_row].set(c_ctx)
    rope_tabs = _rope_tables(ns)

    xp, xs = x_prompt, x_sample
    new_k, new_v = [], []
    for l in range(depth):
        lambda_init = 0.8 - 0.6 * math.exp(-0.3 * l)
        mod = _modulation(cond, w_ada[l], b_ada[l]).reshape(MOD_ROWS * 6, 1, D_MODEL)
        w_in_l, w_out_l = w_in[l].astype(BF16), w_out[l].astype(BF16)
        w_ffn_in_l, w_ffn_out_l = w_ffn_in[l].astype(BF16), w_ffn_out[l].astype(BF16)
        lam_params = jnp.stack([lambda_q1[l], lambda_k1[l], lambda_q2[l], lambda_k2[l]])

        groups = (
            (xp, lambda b: ctx_row, None, None, sp, sp),
            (xs, lambda b: b, rope_tabs,
             (cache_k[:, l].reshape(bs, past, QK_WIDTH), cache_v[:, l].reshape(bs, past, ATTN_WIDTH)),
             512, 256),
        )
        outs = []
        for x, mod_row, tabs, cache, tm, tq in groups:
            emit_state = cache is None
            res = _inproj(x, mod, mod_row, g_attn_pre[l], w_in_l, conv_w[l], conv_b[l], tabs,
                          tm=tm, emit_state=emit_state)
            q, kT, v, co = res[:4]
            if emit_state:
                new_k.append(res[4].reshape(bp, sp, N_HEADS, 2 * HEAD_DIM))
                new_v.append(res[5].reshape(bp, sp, N_HEADS, V_DIM))
            x1 = _attention(q, kT, v, cache, co, x, mod, mod_row, lam_params, g_subln[l],
                            g_attn_post[l], w_out_l, tq=tq, lambda_init=lambda_init)
            outs.append(_ffn(x1, mod, mod_row, g_ffn_pre[l], g_ffn_post[l], w_ffn_in_l,
                             w_ffn_out_l, tm=min(512, x.shape[1])))
        xp, xs = outs

    return (xp, xs, jnp.stack(new_k, axis=1), jnp.stack(new_v, axis=1))
```

```python
import functools
import math

import numpy as np
import jax
import jax.numpy as jnp
from jax import lax
from jax.experimental import pallas as pl
from jax.experimental.pallas import tpu as pltpu

D_MODEL = 1024
GRID_W = 64
N_HEADS = 4
HEAD_DIM = 64
V_DIM = 2 * HEAD_DIM
QK_WIDTH = N_HEADS * 2 * HEAD_DIM
ATTN_WIDTH = N_HEADS * V_DIM
CONV_WIDTH = D_MODEL - ATTN_WIDTH
CONV_K = 3
IN_WIDTH = 2 * QK_WIDTH + ATTN_WIDTH + 3 * CONV_WIDTH
D_FF = -(-8 * D_MODEL // (3 * 256)) * 256
ROPE_THETA = 10000.0
EPS = 1e-6
ATTN_SCALE = HEAD_DIM ** -0.5
Q_PRESCALE = ATTN_SCALE * math.log2(math.e)

V7X_LANES = 128
BF16_SUBLANE_TILE = 16
V7X_VMEM_LIMIT_BYTES = 56 * 1024 * 1024
NEG_BIG = -1e30

F32 = jnp.float32
BF16 = jnp.bfloat16

MOD_SHIFT_A, MOD_SCALE_A, MOD_GATE_A, MOD_SHIFT_F, MOD_SCALE_F, MOD_GATE_F = range(6)
MOD_ROWS = 16


def _params(n_grid_axes):
    return pltpu.CompilerParams(
        dimension_semantics=("arbitrary",) * n_grid_axes,
        vmem_limit_bytes=V7X_VMEM_LIMIT_BYTES)


def _rms(x):
    return lax.rsqrt(jnp.mean(x * x, axis=-1, keepdims=True) + EPS)


def _norm_mod(x, g, scale, shift):
    y = x * _rms(x) * g
    return (y * (1.0 + scale) + shift).astype(BF16)


def _dot(a, b):
    return jnp.dot(a, b, preferred_element_type=F32)


def _mod_kernel(cond_ref, w_ref, b_ref, o_ref):
    c = cond_ref[...]
    a = c * jax.nn.sigmoid(c)
    o_ref[...] = jnp.dot(a, w_ref[...], preferred_element_type=F32,
                         precision=lax.Precision.HIGHEST) + b_ref[...]


def _modulation(cond, w_ada, b_ada):
    n_out = w_ada.shape[1]
    bn = D_MODEL
    return pl.pallas_call(
        _mod_kernel,
        out_shape=jax.ShapeDtypeStruct((MOD_ROWS, n_out), F32),
        grid=(n_out // bn,),
        in_specs=[pl.BlockSpec((MOD_ROWS, D_MODEL), lambda j: (0, 0)),
                  pl.BlockSpec((D_MODEL, bn), lambda j: (0, j)),
                  pl.BlockSpec((1, bn), lambda j: (0, j))],
        out_specs=pl.BlockSpec((MOD_ROWS, bn), lambda j: (0, j)),
        compiler_params=_params(1),
        name="modulation",
    )(cond, w_ada, b_ada.reshape(1, n_out))


def _rope(xf, cos, sin_signed, first_half):
    outs = []
    for j in range(xf.shape[1] // V7X_LANES):
        xj = xf[:, V7X_LANES * j:V7X_LANES * (j + 1)]
        partner = jnp.where(first_half,
                            pltpu.roll(xj, V7X_LANES - 16, 1),
                            pltpu.roll(xj, 16, 1))
        outs.append(xj * cos + partner * sin_signed)
    return jnp.concatenate(outs, axis=1)


def _inproj_kernel(*refs, tm, n_tiles, rotary, emit_state):
    halo = n_tiles > 1
    it = iter(refs)
    x_ref = next(it)
    xprev_ref = next(it) if halo else None
    xnext_ref = next(it) if halo else None
    shift_ref, scale_ref, g_ref, w_ref, cw_ref, cb_ref = (next(it) for _ in range(6))
    cos_ref = next(it) if rotary else None
    sin_ref = next(it) if rotary else None
    qT_ref, k_ref, vT_ref, co_ref = (next(it) for _ in range(4))
    kf_ref = next(it) if emit_state else None
    vf_ref = next(it) if emit_state else None
    z_ref = next(it)

    H = BF16_SUBLANE_TILE
    g, scale, shift = g_ref[...], scale_ref[0], shift_ref[0]
    if halo:
        xe = jnp.concatenate([xprev_ref[...], x_ref[...], xnext_ref[...]], axis=0)
        ue = _norm_mod(xe, g, scale, shift)
        u = ue[H:H + tm]
    else:
        ue = u = _norm_mod(x_ref[...], g, scale, shift)

    def proj(lhs, part):
        return _dot(lhs, w_ref[:, part * QK_WIDTH:(part + 1) * QK_WIDTH])

    qf, kf, vf, bg = proj(u, 0), proj(u, 1), proj(u, 2), proj(u, 3)
    if emit_state:
        kf_ref[...] = kf
        vf_ref[...] = vf
    if rotary:
        lane = lax.broadcasted_iota(jnp.int32, (tm, V7X_LANES), 1)
        first_half = (lane & 16) == 0
        cos, sin = cos_ref[...], sin_ref[...]
        qf = _rope(qf, cos, sin, first_half)
        kf = _rope(kf, cos, sin, first_half)
    qT_ref[...] = (qf * Q_PRESCALE).T.astype(BF16)
    k_ref[...] = kf.astype(BF16)
    vT_ref[...] = vf.T.astype(BF16)

    z = proj(ue, 4) * proj(ue, 5)
    if halo:
        i = pl.program_id(1)
        row = lax.broadcasted_iota(jnp.int32, (tm + 2 * H, 1), 0)
        keep = ((row >= H) | (i > 0)) & ((row < tm + H) | (i < n_tiles - 1))
        z_ref[...] = jnp.where(keep, z, 0.0)
    else:
        zeros = jnp.zeros((H, CONV_WIDTH), F32)
        z_ref[0:H] = zeros
        z_ref[H:H + tm] = z
        z_ref[H + tm:tm + 2 * H] = zeros
    cw = cw_ref[...]
    y = (cw[0:1] * z_ref[H - 1:H - 1 + tm] + cw[1:2] * z_ref[H:H + tm]
         + cw[2:3] * z_ref[H + 1:H + 1 + tm] + cb_ref[...])
    co_ref[...] = (bg * y).astype(BF16)


def _inproj(x, mod, mod_row, g, w_in, conv_w, conv_b, rope_tabs, *, tm, emit_state):
    nb, n, _ = x.shape
    n_tiles = n // tm
    halo = n_tiles > 1
    rotary = rope_tabs is not None
    H = BF16_SUBLANE_TILE
    hb = tm // H

    def mod_spec(comp):
        return pl.BlockSpec((1, 1, D_MODEL), lambda b, i: (mod_row(b) * 6 + comp, 0, 0))

    def full(shape):
        return pl.BlockSpec(shape, lambda b, i: (0,) * len(shape))

    in_specs = [pl.BlockSpec((None, tm, D_MODEL), lambda b, i: (b, i, 0))]
    args = [x]
    if halo:
        in_specs += [
            pl.BlockSpec((None, H, D_MODEL), lambda b, i: (b, jnp.maximum(i * hb - 1, 0), 0)),
            pl.BlockSpec((None, H, D_MODEL),
                         lambda b, i: (b, jnp.minimum((i + 1) * hb, n // H - 1), 0))]
        args += [x, x]
    in_specs += [mod_spec(MOD_SHIFT_A), mod_spec(MOD_SCALE_A), full((1, D_MODEL)),
                 full((D_MODEL, IN_WIDTH)), full((CONV_K, CONV_WIDTH)), full((1, CONV_WIDTH))]
    args += [mod, mod, g.reshape(1, D_MODEL), w_in, conv_w, conv_b.reshape(1, CONV_WIDTH)]
    if rotary:
        in_specs += [pl.BlockSpec((tm, V7X_LANES), lambda b, i: (i, 0))] * 2
        args += list(rope_tabs)

    tok = lambda width, dt: jax.ShapeDtypeStruct((nb, n, width), dt)
    tok_spec = lambda width: pl.BlockSpec((None, tm, width), lambda b, i: (b, i, 0))
    tokT = lambda width: jax.ShapeDtypeStruct((nb, width, n), BF16)
    tokT_spec = lambda width: pl.BlockSpec((None, width, tm), lambda b, i: (b, 0, i))
    out_shape = [tokT(QK_WIDTH), tok(QK_WIDTH, BF16), tokT(ATTN_WIDTH), tok(CONV_WIDTH, BF16)]
    out_specs = [tokT_spec(QK_WIDTH), tok_spec(QK_WIDTH), tokT_spec(ATTN_WIDTH),
                 tok_spec(CONV_WIDTH)]
    if emit_state:
        out_shape += [tok(QK_WIDTH, F32), tok(ATTN_WIDTH, F32)]
        out_specs += [tok_spec(QK_WIDTH), tok_spec(ATTN_WIDTH)]

    return pl.pallas_call(
        functools.partial(_inproj_kernel, tm=tm, n_tiles=n_tiles, rotary=rotary,
                          emit_state=emit_state),
        out_shape=out_shape,
        grid=(nb, n_tiles),
        in_specs=in_specs,
        out_specs=out_specs,
        scratch_shapes=[pltpu.VMEM((tm + 2 * H, CONV_WIDTH), F32)],
        compiler_params=_params(2),
        name="inproj_latent" if rotary else "inproj_context",
    )(*args)


QK_LEAD = 2


def _attn_kernel(*refs, tq, kc, has_cache, lambda_init):
    it = iter(refs)
    qT_ref, k_ref, vT_ref = next(it), next(it), next(it)
    ck_ref = next(it) if has_cache else None
    cv_ref = next(it) if has_cache else None
    (co_ref, x_ref, gate_ref, lam_ref, gsub_ref, gpost_ref, wout_ref, o_ref) = (
        next(it) for _ in range(8))
    cvT_ref = next(it) if has_cache else None

    if has_cache:
        @pl.when(pl.program_id(1) == 0)
        def _():
            cvT_ref[...] = cv_ref[...].T.astype(BF16)

    lp = lam_ref[...]
    lam = (jnp.exp(jnp.sum(lp[0:1] * lp[1:2], axis=-1, keepdims=True))
           - jnp.exp(jnp.sum(lp[2:3] * lp[3:4], axis=-1, keepdims=True)) + lambda_init)

    n_keys = k_ref.shape[0]
    feat = lax.broadcasted_iota(jnp.int32, (V_DIM, tq), 0)
    map0 = feat < HEAD_DIM
    gsub = gsub_ref[...] * (1.0 - lambda_init)

    heads = []
    for h in range(N_HEADS):
        rows = slice(V_DIM * h, V_DIM * (h + 1))
        qh = qT_ref[rows, :]
        zero = jnp.zeros_like(qh)
        q2 = jnp.concatenate([jnp.where(map0, qh, zero), jnp.where(map0, zero, qh)], axis=1)

        def scores(c):
            if has_cache and c == 0:
                return _dot(ck_ref[:, rows].astype(BF16), q2)
            c -= int(has_cache)
            return _dot(k_ref[c * kc:(c + 1) * kc, rows], q2)

        def values(c):
            if has_cache and c == 0:
                return cvT_ref[rows, :]
            c -= int(has_cache)
            return vT_ref[rows, c * kc:(c + 1) * kc]

        n_chunks = n_keys // kc + int(has_cache)
        s_queue = [scores(c) for c in range(min(QK_LEAD, n_chunks))]
        m = jnp.full((1, 2 * tq), NEG_BIG, F32)
        l = jnp.zeros((1, 2 * tq), F32)
        acc = jnp.zeros((V_DIM, 2 * tq), F32)
        for c in range(n_chunks):
            if c + QK_LEAD < n_chunks:
                s_queue.append(scores(c + QK_LEAD))
            s = s_queue[c]
            m_new = jnp.maximum(m, jnp.max(s, axis=0, keepdims=True))
            p = jnp.exp2(s - m_new)
            alpha = jnp.exp2(m - m_new)
            l = alpha * l + jnp.sum(p, axis=0, keepdims=True)
            acc = alpha * acc + _dot(values(c), p.astype(BF16))
            m = m_new

        on = acc * (1.0 / l)
        o = on[:, :tq] - lam * on[:, tq:]
        r = lax.rsqrt(jnp.mean(o * o, axis=0, keepdims=True) + EPS)
        heads.append(o * r * gsub)

    ao = jnp.concatenate(heads, axis=0).T.astype(BF16)
    mix = _dot(jnp.concatenate([ao, co_ref[...]], axis=1), wout_ref[...])
    o_ref[...] = x_ref[...] + gate_ref[0] * (mix * _rms(mix) * gpost_ref[...])


def _attention(qT, k, vT, cache, co, x, mod, mod_row, lam_params, g_subln, g_post, w_out,
               *, tq, kc, lambda_init):
    nb, n, _ = k.shape
    has_cache = cache is not None

    def full(shape):
        return pl.BlockSpec(shape, lambda b, i: (0,) * len(shape))

    tok_spec = lambda width: pl.BlockSpec((None, tq, width), lambda b, i: (b, i, 0))
    in_specs = [pl.BlockSpec((None, QK_WIDTH, tq), lambda b, i: (b, 0, i)),
                pl.BlockSpec((None, n, QK_WIDTH), lambda b, i: (b, 0, 0)),
                pl.BlockSpec((None, ATTN_WIDTH, n), lambda b, i: (b, 0, 0))]
    args = [qT, k, vT]
    scratch = []
    if has_cache:
        past = cache[0].shape[1]
        in_specs += [pl.BlockSpec((None, past, QK_WIDTH), lambda b, i: (b, 0, 0)),
                     pl.BlockSpec((None, past, ATTN_WIDTH), lambda b, i: (b, 0, 0))]
        args += list(cache)
        scratch = [pltpu.VMEM((ATTN_WIDTH, past), BF16)]
    in_specs += [tok_spec(CONV_WIDTH), tok_spec(D_MODEL),
                 pl.BlockSpec((1, 1, D_MODEL), lambda b, i: (mod_row(b) * 6 + MOD_GATE_A, 0, 0)),
                 full((4, HEAD_DIM)), full((V_DIM, 1)), full((1, D_MODEL)),
                 full((D_MODEL, D_MODEL))]
    args += [co, x, mod, lam_params, g_subln.reshape(V_DIM, 1), g_post.reshape(1, D_MODEL), w_out]

    return pl.pallas_call(
        functools.partial(_attn_kernel, tq=tq, kc=kc, has_cache=has_cache,
                          lambda_init=lambda_init),
        out_shape=jax.ShapeDtypeStruct((nb, n, D_MODEL), F32),
        grid=(nb, n // tq),
        in_specs=in_specs,
        out_specs=tok_spec(D_MODEL),
        scratch_shapes=scratch,
        compiler_params=_params(2),
        name="attention_latent" if has_cache else "attention_context",
    )(*args)


FF_CHUNK = 256


def _ffn_kernel(x_ref, shift_ref, scale_ref, gate_ref, gpre_ref, gpost_ref, win_ref, wout_ref,
                o_ref, h_ref):
    x = x_ref[...]
    u = _norm_mod(x, gpre_ref[...], scale_ref[0], shift_ref[0])
    for c in range(D_FF // FF_CHUNK):
        lo = c * FF_CHUNK
        a = _dot(u, win_ref[:, lo:lo + FF_CHUNK])
        b = _dot(u, win_ref[:, D_FF + lo:D_FF + lo + FF_CHUNK])
        h_ref[:, lo:lo + FF_CHUNK] = (a * jax.nn.sigmoid(a) * b).astype(BF16)
    y = _dot(h_ref[...], wout_ref[...])
    o_ref[...] = x + gate_ref[0] * (y * _rms(y) * gpost_ref[...])


def _ffn(x, mod, mod_row, g_pre, g_post, w_ffn_in, w_ffn_out, *, tm):
    nb, n, _ = x.shape

    def mod_spec(comp):
        return pl.BlockSpec((1, 1, D_MODEL), lambda b, i: (mod_row(b) * 6 + comp, 0, 0))

    def full(shape):
        return pl.BlockSpec(shape, lambda b, i: (0,) * len(shape))

    tok_spec = pl.BlockSpec((None, tm, D_MODEL), lambda b, i: (b, i, 0))
    return pl.pallas_call(
        _ffn_kernel,
        out_shape=jax.ShapeDtypeStruct((nb, n, D_MODEL), F32),
        grid=(nb, n // tm),
        in_specs=[tok_spec, mod_spec(MOD_SHIFT_F), mod_spec(MOD_SCALE_F), mod_spec(MOD_GATE_F),
                  full((1, D_MODEL)), full((1, D_MODEL)),
                  full((D_MODEL, 2 * D_FF)), full((D_FF, D_MODEL))],
        out_specs=tok_spec,
        scratch_shapes=[pltpu.VMEM((tm, D_FF), BF16)],
        compiler_params=_params(2),
        name="ffn",
    )(x, mod, mod, mod, g_pre.reshape(1, D_MODEL), g_post.reshape(1, D_MODEL),
      w_ffn_in, w_ffn_out)


def _rope_tables(n):
    t = np.arange(n)
    n_freq = HEAD_DIM // 4
    inv = 1.0 / (ROPE_THETA ** (np.arange(n_freq, dtype=np.float64) / n_freq))
    ar = (t // GRID_W)[:, None] * inv[None, :]
    ac = (t % GRID_W)[:, None] * inv[None, :]
    cos = np.concatenate([np.cos(ar), np.cos(ar), np.cos(ac), np.cos(ac)], axis=1)
    sin = np.concatenate([-np.sin(ar), np.sin(ar), -np.sin(ac), np.sin(ac)], axis=1)
    reps = V7X_LANES // HEAD_DIM
    return (jnp.asarray(np.tile(cos, (1, reps)), F32), jnp.asarray(np.tile(sin, (1, reps)), F32))


def kernel(x_prompt, x_sample, cache_k, cache_v, c, c_ctx, w_ada, b_ada, g_attn_pre, g_attn_post,
           g_ffn_pre, g_ffn_post, w_in, conv_w, conv_b, lambda_q1, lambda_k1, lambda_q2,
           lambda_k2, g_subln, w_out, w_ffn_in, w_ffn_out):
    depth = w_ada.shape[0]
    bp, sp, _ = x_prompt.shape
    bs, ns, _ = x_sample.shape
    past = cache_k.shape[2]
    ctx_row = bs
    cond = jnp.zeros((MOD_ROWS, D_MODEL), F32).at[:bs].set(c).at[ctx_row].set(c_ctx)
    rope_tabs = _rope_tables(ns)

    xp, xs = x_prompt, x_sample
    new_k, new_v = [], []
    for l in range(depth):
        lambda_init = 0.8 - 0.6 * math.exp(-0.3 * l)
        mod = _modulation(cond, w_ada[l], b_ada[l]).reshape(MOD_ROWS * 6, 1, D_MODEL)
        w_in_l, w_out_l = w_in[l].astype(BF16), w_out[l].astype(BF16)
        w_ffn_in_l, w_ffn_out_l = w_ffn_in[l].astype(BF16), w_ffn_out[l].astype(BF16)
        lam_params = jnp.stack([lambda_q1[l], lambda_k1[l], lambda_q2[l], lambda_k2[l]])

        groups = (
            (xp, lambda b: ctx_row, None, None, sp, sp),
            (xs, lambda b: b, rope_tabs,
             (cache_k[:, l].reshape(bs, past, QK_WIDTH), cache_v[:, l].reshape(bs, past, ATTN_WIDTH)),
             512, 256),
        )
        outs = []
        for x, mod_row, tabs, cache, tm, tq in groups:
            emit_state = cache is None
            res = _inproj(x, mod, mod_row, g_attn_pre[l], w_in_l, conv_w[l], conv_b[l], tabs,
                          tm=tm, emit_state=emit_state)
            qT, k, vT, co = res[:4]
            if emit_state:
                new_k.append(res[4].reshape(bp, sp, N_HEADS, 2 * HEAD_DIM))
                new_v.append(res[5].reshape(bp, sp, N_HEADS, V_DIM))
            x1 = _attention(qT, k, vT, cache, co, x, mod, mod_row, lam_params, g_subln[l],
                            g_attn_post[l], w_out_l, tq=tq, kc=min(512, x.shape[1]),
                            lambda_init=lambda_init)
            outs.append(_ffn(x1, mod, mod_row, g_ffn_pre[l], g_ffn_post[l], w_ffn_in_l,
                             w_ffn_out_l, tm=min(512, x.shape[1])))
        xp, xs = outs

    return (xp, xs, jnp.stack(new_k, axis=1), jnp.stack(new_v, axis=1))
```
